```python
import math, functools
import jax, jax.numpy as jnp
from jax import lax
import numpy as np

D_MODEL = 2048
BATCH = 4
SEQ = 2048
DEPTH = 1
DEC_BATCH = 32
DEC_SEQ = 1
PAST_LEN = 16384
PAGE_SIZE = 128

NH_A = 4
DH_A = D_MODEL // 8
W_A = NH_A * DH_A
MLSTM_CHUNK = 64
NH_B = 8
DH_B = D_MODEL // 16
W_B = NH_B * DH_B
BLOCK = 256
TOP_K = 3
Q_BLOCK = 16
RMS_EPS = 1e-6
IN_SIZES = (W_A, W_A, W_A, W_A, W_A, NH_A, NH_A, W_B, W_B, W_B, W_B, D_MODEL, D_MODEL)
N_IN = sum(IN_SIZES)

kernel_name = "hybrid_mlstm_moba_decode_step"


def _rmsnorm(x, g):
    xf = x.astype(jnp.float32)
    y = xf * lax.rsqrt(jnp.mean(xf * xf, axis=-1, keepdims=True) + RMS_EPS)
    return (y * g.astype(jnp.float32)).astype(x.dtype)


def _in_projection(x, norm_g, w_in):
    h = _rmsnorm(x, norm_g)
    p = jnp.einsum("bsd,dn->bsn", h, w_in)
    offs = np.cumsum(IN_SIZES)[:-1].tolist()
    return jnp.split(p, offs, axis=-1)


def _mlstm_scan(q, k, v, i_t, logf, c0, n0, m0):
    B, S, H, DH = q.shape
    L = math.gcd(S, MLSTM_CHUNK)
    NC = S // L
    k = k * (DH ** -0.5)

    def chunks(a):
        return jnp.swapaxes(a.reshape((B, NC, L) + a.shape[2:]), 0, 1)

    causal = jnp.tril(jnp.ones((L, L), dtype=bool))

    def step(carry, xs):
        c, n, m = carry
        qc, kc, vc, ic, fc = xs
        b = jnp.cumsum(fc, axis=1)
        logd = b[:, :, None, :] - b[:, None, :, :] + ic[:, None, :, :]
        logd = jnp.where(causal[None, :, :, None], logd, -jnp.inf)
        log_inter = m[:, None, :] + b
        m_t = jnp.maximum(log_inter, jnp.max(logd, axis=2))
        d = jnp.exp(logd - m_t[:, :, None, :])
        w_inter = jnp.exp(log_inter - m_t)
        sw = jnp.einsum("bthd,bshd->btsh", qc, kc) * d
        num = jnp.einsum("btsh,bshd->bthd", sw, vc) + w_inter[..., None] * jnp.einsum("bhvd,bthd->bthv", c, qc)
        den = jnp.sum(sw, axis=2) + w_inter * jnp.einsum("bhd,bthd->bth", n, qc)
        h = num / jnp.maximum(jnp.abs(den), jnp.exp(-m_t))[..., None]
        m_new = m_t[:, -1]
        w_s = jnp.exp(b[:, -1:, :] - b + ic - m_new[:, None, :])
        w_c = jnp.exp(m + b[:, -1] - m_new)
        c_new = w_c[..., None, None] * c + jnp.einsum("bsh,bshv,bshd->bhvd", w_s, vc, kc)
        n_new = w_c[..., None] * n + jnp.einsum("bsh,bshd->bhd", w_s, kc)
        return (c_new, n_new, m_new), h

    xs = (chunks(q), chunks(k), chunks(v), chunks(i_t), chunks(logf))
    (c, n, m), h = lax.scan(step, (c0, n0, m0), xs)
    h = jnp.swapaxes(h, 0, 1).reshape(B, S, H, DH)
    return h, c, n, m


def _mlstm_branch(q, k, v, o, z, i_pre, f_pre, c0, n0, m0, b_i, b_f, hnorm_g):
    B, S, _ = q.shape
    f32 = jnp.float32
    heads = lambda a: a.astype(f32).reshape(B, S, NH_A, DH_A)
    i_t = i_pre.astype(f32) + b_i.astype(f32)
    logf = jax.nn.log_sigmoid(f_pre.astype(f32) + b_f.astype(f32))
    h, c, n, m = _mlstm_scan(heads(q), heads(k), heads(v), i_t, logf,
                             c0.astype(f32), n0.astype(f32), m0.astype(f32))
    h = _rmsnorm(h, hnorm_g.reshape(NH_A, DH_A)).reshape(B, S, W_A).astype(q.dtype)
    return h * jax.nn.sigmoid(o) * jax.nn.silu(z), c, n, m


def _moba_attend(q, k_loc, v_loc, loc_mask, k_sel=None, v_sel=None, sel_valid=None):
    f32 = jnp.float32
    qf = q.astype(f32) * (q.shape[-1] ** -0.5)
    s_loc = jnp.einsum("bqhd,bthd->bqht", qf, k_loc.astype(f32))
    s_loc = jnp.where(loc_mask[None, :, None, :], s_loc, -jnp.inf)
    if k_sel is None:
        p = jax.nn.softmax(s_loc, axis=-1)
        out = jnp.einsum("bqht,bthd->bqhd", p, v_loc.astype(f32))
        return out.astype(q.dtype)
    B, Q, H, K, N, _ = k_sel.shape
    s_sel = jnp.einsum("bqhd,bqhknd->bqhkn", qf, k_sel.astype(f32))
    s_sel = jnp.where(sel_valid[..., None], s_sel, -jnp.inf).reshape(B, Q, H, K * N)
    p = jax.nn.softmax(jnp.concatenate([s_sel, s_loc], axis=-1), axis=-1)
    p_sel = p[..., :K * N].reshape(B, Q, H, K, N)
    p_loc = p[..., K * N:]
    out = (jnp.einsum("bqhkn,bqhknd->bqhd", p_sel, v_sel.astype(f32))
           + jnp.einsum("bqht,bthd->bqhd", p_loc, v_loc.astype(f32)))
    return out.astype(q.dtype)


def _moba_prompt(q, k, v):
    B, S, H, DH = q.shape
    NB = -(-S // BLOCK)
    pad = NB * BLOCK - S
    kp = jnp.pad(k, ((0, 0), (0, pad), (0, 0), (0, 0)))
    vp = jnp.pad(v, ((0, 0), (0, pad), (0, 0), (0, 0)))
    n_cand = NB - 1
    k_n = min(TOP_K, n_cand)
    if k_n > 0:
        kb = jnp.moveaxis(kp.reshape(B, NB, BLOCK, H, DH), 3, 1)
        vb = jnp.moveaxis(vp.reshape(B, NB, BLOCK, H, DH), 3, 1)
        kmean = jnp.mean(kb[:, :, :n_cand], axis=3, dtype=jnp.float32)
        bidx = jnp.arange(B)[:, None, None, None]
        hidx = jnp.arange(H)[None, None, :, None]
    NQ = S // Q_BLOCK
    qs = jnp.swapaxes(q.reshape(B, NQ, Q_BLOCK, H, DH), 0, 1)

    def one(xs):
        qc, qi = xs
        q0 = qi * Q_BLOCK
        own = q0 // BLOCK
        pos_q = q0 + jnp.arange(Q_BLOCK)
        pos_k = own * BLOCK + jnp.arange(BLOCK)
        loc_mask = pos_k[None, :] <= pos_q[:, None]
        k_loc = lax.dynamic_slice_in_dim(kp, own * BLOCK, BLOCK, axis=1)
        v_loc = lax.dynamic_slice_in_dim(vp, own * BLOCK, BLOCK, axis=1)
        if k_n == 0:
            return _moba_attend(qc, k_loc, v_loc, loc_mask)
        scores = jnp.einsum("bqhd,bhnd->bqhn", qc.astype(jnp.float32), kmean)
        scores = jnp.where(jnp.arange(n_cand) < own, scores, -jnp.inf)
        _, top_i = lax.top_k(scores, k_n)
        valid = (jnp.arange(k_n) < own)[None, None, None, :]
        k_sel = kb[bidx, hidx, top_i]
        v_sel = vb[bidx, hidx, top_i]
        return _moba_attend(qc, k_loc, v_loc, loc_mask, k_sel, v_sel, valid)

    out = lax.map(one, (qs, jnp.arange(NQ)))
    return jnp.swapaxes(out, 0, 1).reshape(B, S, H * DH)


def _moba_sample(q, k_new, v_new, cache_k, cache_v, page_table):
    DB, DS, H, DH = q.shape
    f32 = jnp.float32
    ppb = BLOCK // PAGE_SIZE
    nbc = PAST_LEN // BLOCK
    tail0 = nbc * BLOCK
    n_tail_pages = (PAST_LEN - tail0) // PAGE_SIZE
    k_tail, v_tail = k_new, v_new
    if n_tail_pages > 0:
        pt = page_table[:, nbc * ppb: nbc * ppb + n_tail_pages]
        kc_t = cache_k[pt].reshape(DB, n_tail_pages * PAGE_SIZE, H, DH).astype(k_new.dtype)
        vc_t = cache_v[pt].reshape(DB, n_tail_pages * PAGE_SIZE, H, DH).astype(v_new.dtype)
        k_tail = jnp.concatenate([kc_t, k_new], axis=1)
        v_tail = jnp.concatenate([vc_t, v_new], axis=1)
    R = PAST_LEN - tail0 + DS
    nbt = -(-R // BLOCK)
    k_tail = jnp.pad(k_tail, ((0, 0), (0, nbt * BLOCK - R), (0, 0), (0, 0)))
    v_tail = jnp.pad(v_tail, ((0, 0), (0, nbt * BLOCK - R), (0, 0), (0, 0)))
    pos_q = PAST_LEN + jnp.arange(DS)
    pos_t = tail0 + jnp.arange(nbt * BLOCK)
    own_q = pos_q // BLOCK
    loc_mask = ((pos_t[None, :] // BLOCK) == own_q[:, None]) & (pos_t[None, :] <= pos_q[:, None])
    n_cand = nbc + nbt - 1
    k_n = min(TOP_K, n_cand)
    if k_n == 0:
        return _moba_attend(q, k_tail, v_tail, loc_mask).reshape(DB, DS, H * DH)
    means = []
    if nbc > 0:
        page_mean = jnp.mean(cache_k, axis=1, dtype=f32)
        pm = page_mean[page_table[:, :nbc * ppb]]
        means.append(jnp.mean(pm.reshape(DB, nbc, ppb, H, DH), axis=2))
    if nbt > 1:
        kt_blocks = k_tail.reshape(DB, nbt, BLOCK, H, DH)
        vt_blocks = v_tail.reshape(DB, nbt, BLOCK, H, DH)
        means.append(jnp.mean(kt_blocks[:, :nbt - 1], axis=2, dtype=f32))
    kmean = jnp.concatenate(means, axis=1)
    scores = jnp.einsum("bqhd,bnhd->bqhn", q.astype(f32), kmean)
    cand = jnp.arange(n_cand)[None, :] < own_q[:, None]
    scores = jnp.where(cand[None, :, None, :], scores, -jnp.inf)
    _, top_i = lax.top_k(scores, k_n)
    valid = (jnp.arange(k_n)[None, :] < own_q[:, None])[None, :, None, :]
    bidx = jnp.arange(DB)[:, None, None, None]
    hidx = jnp.arange(H)[None, None, :, None]
    if nbc > 0:
        ic = jnp.minimum(top_i, nbc - 1)
        pages = page_table[bidx[..., None], ic[..., None] * ppb + jnp.arange(ppb)]
        hsel = hidx[..., None]
        kc = cache_k[pages, :, hsel].reshape(DB, DS, H, k_n, BLOCK, DH).astype(k_new.dtype)
        vc = cache_v[pages, :, hsel].reshape(DB, DS, H, k_n, BLOCK, DH).astype(v_new.dtype)
    if nbt > 1:
        it = jnp.clip(top_i - nbc, 0, nbt - 2)
        kt = jnp.moveaxis(kt_blocks, 3, 1)[bidx, hidx, it]
        vt = jnp.moveaxis(vt_blocks, 3, 1)[bidx, hidx, it]
    if nbc > 0 and nbt > 1:
        is_c = (top_i < nbc)[..., None, None]
        k_sel = jnp.where(is_c, kc, kt)
        v_sel = jnp.where(is_c, vc, vt)
    elif nbc > 0:
        k_sel, v_sel = kc, vc
    else:
        k_sel, v_sel = kt, vt
    out = _moba_attend(q, k_tail, v_tail, loc_mask, k_sel, v_sel, valid)
    return out.reshape(DB, DS, H * DH)


def _layer(x, c0, n0, m0, moba_fn, norm_g, w_in, b_i, b_f, hnorm_g, qn_g, kn_g, w_a, w_b, w_out):
    B, S, _ = x.shape
    (qa, ka, va, oa, za, ia, fa, qb, kb, vb, zb, ga, gb) = _in_projection(x, norm_g, w_in)
    h_a, c, n, m = _mlstm_branch(qa, ka, va, oa, za, ia, fa, c0, n0, m0, b_i, b_f, hnorm_g)
    qb = _rmsnorm(qb.reshape(B, S, NH_B, DH_B), qn_g)
    kb = _rmsnorm(kb.reshape(B, S, NH_B, DH_B), kn_g)
    vb = vb.reshape(B, S, NH_B, DH_B)
    h_b = moba_fn(qb, kb, vb) * jax.nn.silu(zb)
    u = (jax.nn.sigmoid(ga) * jnp.einsum("bsw,wd->bsd", h_a, w_a)
         + jax.nn.sigmoid(gb) * jnp.einsum("bsw,wd->bsd", h_b, w_b))
    y = x + jnp.einsum("bsd,de->bse", u, w_out)
    return y, kb, vb, c, n, m


def setup_inputs(seed: int = 0) -> dict:
    key = jax.random.key(seed)
    ks = jax.random.split(key, 18)
    f32 = jnp.float32
    n_pages = PAST_LEN // PAGE_SIZE
    n_pool = (DEC_BATCH * n_pages * 5) // 4

    def nrm(k, shape, scale=1.0):
        return scale * jax.random.normal(k, shape, f32)

    page_table = jax.random.permutation(ks[7], n_pool)[: DEC_BATCH * n_pages]
    page_table = page_table.reshape(DEC_BATCH, n_pages).astype(jnp.int32)
    return {
        "x_prompt": nrm(ks[0], (BATCH, SEQ, D_MODEL)),
        "x_sample": nrm(ks[1], (DEC_BATCH, DEC_SEQ, D_MODEL)),
        "cache_k": nrm(ks[2], (DEPTH, n_pool, PAGE_SIZE, NH_B, DH_B)),
        "cache_v": nrm(ks[3], (DEPTH, n_pool, PAGE_SIZE, NH_B, DH_B)),
        "state_c": nrm(ks[4], (DEPTH, DEC_BATCH, NH_A, DH_A, DH_A)),
        "state_n": nrm(ks[5], (DEPTH, DEC_BATCH, NH_A, DH_A)),
        "state_m": nrm(ks[6], (DEPTH, DEC_BATCH, NH_A), 0.5),
        "page_table": page_table,
        "norm_g": 1.0 + nrm(ks[8], (DEPTH, D_MODEL), 0.1),
        "w_in": nrm(ks[9], (DEPTH, D_MODEL, N_IN), D_MODEL ** -0.5),
        "b_i": nrm(ks[10], (DEPTH, NH_A), 0.1),
        "b_f": 3.0 + nrm(ks[11], (DEPTH, NH_A), 0.5),
        "mlstm_norm_g": 1.0 + nrm(ks[12], (DEPTH, W_A), 0.1),
        "q_norm_g": 1.0 + nrm(ks[13], (DEPTH, DH_B), 0.1),
        "k_norm_g": 1.0 + nrm(ks[14], (DEPTH, DH_B), 0.1),
        "w_a": nrm(ks[15], (DEPTH, W_A, D_MODEL), W_A ** -0.5),
        "w_b": nrm(ks[16], (DEPTH, W_B, D_MODEL), W_B ** -0.5),
        "w_out": nrm(ks[17], (DEPTH, D_MODEL, D_MODEL), D_MODEL ** -0.5),
    }


def reference(x_prompt, x_sample, cache_k, cache_v, state_c, state_n, state_m, page_table,
              norm_g, w_in, b_i, b_f, mlstm_norm_g, q_norm_g, k_norm_g, w_a, w_b, w_out):
    yp, ys = x_prompt, x_sample
    kp_l, vp_l, cp_l, np_l, mp_l = [], [], [], [], []
    ks_l, vs_l, cs_l, ns_l, ms_l = [], [], [], [], []
    B = x_prompt.shape[0]
    for l in range(DEPTH):
        w = (norm_g[l], w_in[l], b_i[l], b_f[l], mlstm_norm_g[l], q_norm_g[l], k_norm_g[l],
             w_a[l], w_b[l], w_out[l])
        c0 = jnp.zeros((B, NH_A, DH_A, DH_A), jnp.float32)
        n0 = jnp.zeros((B, NH_A, DH_A), jnp.float32)
        m0 = jnp.zeros((B, NH_A), jnp.float32)
        yp, kpr, vpr, cpr, npr, mpr = _layer(yp, c0, n0, m0, _moba_prompt, *w)
        moba_s = functools.partial(_moba_sample, cache_k=cache_k[l], cache_v=cache_v[l],
                                   page_table=page_table)
        ys, ksr, vsr, csr, nsr, msr = _layer(ys, state_c[l], state_n[l], state_m[l], moba_s, *w)
        kp_l.append(kpr); vp_l.append(vpr); cp_l.append(cpr); np_l.append(npr); mp_l.append(mpr)
        ks_l.append(ksr); vs_l.append(vsr); cs_l.append(csr); ns_l.append(nsr); ms_l.append(msr)
    return (yp, ys,
            jnp.stack(kp_l), jnp.stack(vp_l), jnp.stack(cp_l), jnp.stack(np_l), jnp.stack(mp_l),
            jnp.stack(ks_l), jnp.stack(vs_l), jnp.stack(cs_l), jnp.stack(ns_l), jnp.stack(ms_l))
```

```python
import functools

import jax
import jax.numpy as jnp
from jax import lax
from jax.experimental import pallas as pl
from jax.experimental.pallas import tpu as pltpu

F32 = jnp.float32
BF16 = jnp.bfloat16

RMS_EPS = 1e-6
NH_A = 4
NH_B = 8
BLOCK = 256
TOP_K = 3
PAGE_SIZE = 128
MLSTM_CHUNK = 256
LANES = 128
VMEM_LIMIT = 56 * 1024 * 1024

NEG_INF = float("-inf")


def _cparams(*sem):
    return pltpu.CompilerParams(dimension_semantics=sem, vmem_limit_bytes=VMEM_LIMIT)


def _sigmoid(x):
    return 1.0 / (1.0 + jnp.exp(-x))


def _log_sigmoid(x):
    return jnp.minimum(x, 0.0) - jnp.log1p(jnp.exp(-jnp.abs(x)))


def _dot(a, b):
    return jnp.dot(a, b, preferred_element_type=F32)


def _dot_nt(a, b):
    return lax.dot_general(a, b, (((1,), (1,)), ((), ())), preferred_element_type=F32)


def _dot_tn(a, b):
    return lax.dot_general(a, b, (((0,), (0,)), ((), ())), preferred_element_type=F32)


class _Cols:
    def __init__(self, d_model):
        w_a = d_model // 2
        w_b = d_model // 2
        self.w_a, self.w_b, self.d = w_a, w_b, d_model
        off = 0
        for name, size in (("ga", d_model), ("gb", d_model), ("qa", w_a), ("ka", w_a), ("va", w_a),
                           ("oa", w_a), ("za", w_a), ("qb", w_b), ("kb", w_b), ("vb", w_b), ("zb", w_b)):
            setattr(self, name, off)
            off += size
        self.n = off


def _prep_w_in(w_in, d_model):
    w_a = d_model // 2
    o_if = 5 * w_a
    o_b = o_if + 2 * NH_A
    o_g = o_b + 4 * w_a
    w_main = jnp.concatenate([w_in[:, o_g:], w_in[:, :o_if], w_in[:, o_b:o_g]], axis=1).astype(BF16)
    w_if = jnp.pad(w_in[:, o_if:o_b], ((0, 0), (0, LANES - 2 * NH_A))).astype(BF16)
    return w_main, w_if


def _in_proj_kernel(x_ref, g_ref, w_ref, wif_ref, p_ref, if_ref, h_scr, *, row_chunk):
    tm = x_ref.shape[0]

    @pl.when(pl.program_id(1) == 0)
    def _():
        def body(r, carry):
            rows = pl.ds(pl.multiple_of(r * row_chunk, row_chunk), row_chunk)
            x = x_ref[rows, :]
            ms = jnp.mean(x * x, axis=-1, keepdims=True)
            h_scr[rows, :] = (x * lax.rsqrt(ms + RMS_EPS) * g_ref[...]).astype(BF16)
            return carry

        lax.fori_loop(0, tm // row_chunk, body, 0)
        if_ref[...] = _dot(h_scr[...], wif_ref[...])

    p_ref[...] = _dot(h_scr[...], w_ref[...])


def _in_proj(x2d, norm_g, w_main, w_if, tm, tn):
    m, d = x2d.shape
    n = w_main.shape[1]
    row_chunk = min(tm, 128)
    return pl.pallas_call(
        functools.partial(_in_proj_kernel, row_chunk=row_chunk),
        grid=(m // tm, n // tn),
        in_specs=[
            pl.BlockSpec((tm, d), lambda i, j: (i, 0)),
            pl.BlockSpec((1, d), lambda i, j: (0, 0)),
            pl.BlockSpec((d, tn), lambda i, j: (0, j)),
            pl.BlockSpec((d, LANES), lambda i, j: (0, 0)),
        ],
        out_specs=[
            pl.BlockSpec((tm, tn), lambda i, j: (i, j)),
            pl.BlockSpec((tm, LANES), lambda i, j: (i, 0)),
        ],
        out_shape=[jax.ShapeDtypeStruct((m, n), F32), jax.ShapeDtypeStruct((m, LANES), F32)],
        scratch_shapes=[pltpu.VMEM((tm, d), BF16)],
        compiler_params=_cparams("parallel", "arbitrary"),
        name="in_proj",
    )(x2d, norm_g.reshape(1, d), w_main, w_if)


def _head_out(h, g, o, z):
    hn = h * lax.rsqrt(jnp.mean(h * h, axis=-1, keepdims=True) + RMS_EPS) * g
    return hn * _sigmoid(o) * (z * _sigmoid(z))


def _mlstm_prompt_kernel(q_ref, k_ref, v_ref, o_ref, z_ref, if_ref, bias_ref, g_ref,
                         h_ref, c_ref, n_ref, m_ref):
    L = q_ref.shape[0]
    dh = q_ref.shape[1] // NH_A

    @pl.when(pl.program_id(1) == 0)
    def _():
        c_ref[...] = jnp.zeros_like(c_ref)
        n_ref[...] = jnp.zeros_like(n_ref)
        m_ref[...] = jnp.zeros_like(m_ref)

    t_idx = lax.broadcasted_iota(jnp.int32, (L, L), 0)
    s_idx = lax.broadcasted_iota(jnp.int32, (L, L), 1)
    eye = t_idx == s_idx
    causal = s_idx <= t_idx

    def to_row(col):
        return jnp.sum(jnp.where(eye, col, 0.0), axis=0, keepdims=True)

    gates = if_ref[...] + bias_ref[...]
    for h in range(NH_A):
        cols = slice(h * dh, (h + 1) * dh)
        q = q_ref[:, cols]
        ks = k_ref[:, cols] * (dh ** -0.5)
        v = v_ref[:, cols]
        i_col = gates[:, h:h + 1]
        logf_col = _log_sigmoid(gates[:, NH_A + h:NH_A + h + 1])
        m_prev = m_ref[h, :, 0:1]
        c_prev = c_ref[h]
        n_prev = n_ref[h]

        b_col = jnp.sum(jnp.where(causal, to_row(logf_col), 0.0), axis=1, keepdims=True)
        r_row = to_row(b_col - i_col)
        logd = jnp.where(causal, b_col - r_row, NEG_INF)
        log_inter = m_prev + b_col
        m_t = jnp.maximum(log_inter, jnp.max(logd, axis=1, keepdims=True))
        d = jnp.exp(logd - m_t)
        w_inter = jnp.exp(log_inter - m_t)

        qb = q.astype(BF16)
        sw = _dot_nt(qb, ks.astype(BF16)) * d
        num = _dot(sw.astype(BF16), v.astype(BF16)) + w_inter * _dot_nt(qb, c_prev.astype(BF16))
        den = jnp.sum(sw, axis=1, keepdims=True) + w_inter * jnp.sum(q * n_prev, axis=1, keepdims=True)
        hh = num / jnp.maximum(jnp.abs(den), jnp.exp(-m_t))

        m_new = m_t[L - 1:L, :]
        b_last = b_col[L - 1:L, :]
        w_s = jnp.exp(b_last - b_col + i_col - m_new)
        w_c = jnp.exp(m_prev + b_last - m_new)
        kw = ks * w_s
        c_ref[h] = w_c * c_prev + _dot_tn(v.astype(BF16), kw.astype(BF16))
        n_ref[h] = w_c * n_prev + jnp.sum(kw, axis=0, keepdims=True)
        m_ref[h] = jnp.broadcast_to(m_new, (1, LANES))

        h_ref[:, cols] = _head_out(hh, g_ref[:, cols], o_ref[:, cols], z_ref[:, cols])


def _mlstm_prompt(p, gates, bias_if, hnorm_g, cols, batch, seq):
    w_a = cols.w_a
    dh = w_a // NH_A
    L = MLSTM_CHUNK if seq % MLSTM_CHUNK == 0 else 64
    assert seq % L == 0
    nc = seq // L

    def pspec(off):
        return pl.BlockSpec((L, w_a), lambda b, c, off=off: (b * nc + c, off // w_a))

    return pl.pallas_call(
        _mlstm_prompt_kernel,
        grid=(batch, nc),
        in_specs=[pspec(cols.qa), pspec(cols.ka), pspec(cols.va), pspec(cols.oa), pspec(cols.za),
                  pl.BlockSpec((L, LANES), lambda b, c: (b * nc + c, 0)),
                  pl.BlockSpec((1, LANES), lambda b, c: (0, 0)),
                  pl.BlockSpec((1, w_a), lambda b, c: (0, 0))],
        out_specs=[
            pl.BlockSpec((L, w_a), lambda b, c: (b * nc + c, 0)),
            pl.BlockSpec((None, NH_A, dh, dh), lambda b, c: (b, 0, 0, 0)),
            pl.BlockSpec((None, NH_A, 1, dh), lambda b, c: (b, 0, 0, 0)),
            pl.BlockSpec((None, NH_A, 1, LANES), lambda b, c: (b, 0, 0, 0)),
        ],
        out_shape=[
            jax.ShapeDtypeStruct((batch * seq, w_a), F32),
            jax.ShapeDtypeStruct((batch, NH_A, dh, dh), F32),
            jax.ShapeDtypeStruct((batch, NH_A, 1, dh), F32),
            jax.ShapeDtypeStruct((batch, NH_A, 1, LANES), F32),
        ],
        compiler_params=_cparams("parallel", "arbitrary"),
        name="mlstm_prompt",
    )(p, p, p, p, p, gates, bias_if, hnorm_g.reshape(1, w_a))


def _mlstm_sample_kernel(q_ref, k_ref, v_ref, o_ref, z_ref, if_ref, bias_ref, g_ref, c0_ref, n0_ref, m0_ref,
                         h_ref, c_ref, n_ref, m_ref):
    b = pl.program_id(0)
    dh = c0_ref.shape[-1]
    row = pl.ds(b, 1)
    t_idx = lax.broadcasted_iota(jnp.int32, (dh, dh), 0)
    s_idx = lax.broadcasted_iota(jnp.int32, (dh, dh), 1)
    eye = t_idx == s_idx

    gates = if_ref[row, :] + bias_ref[...]
    m0_all = m0_ref[row, :]
    for h in range(NH_A):
        cols = slice(h * dh, (h + 1) * dh)
        q = q_ref[row, cols]
        ks = k_ref[row, cols] * (dh ** -0.5)
        v = v_ref[row, cols]
        c0 = c0_ref[h]
        n0 = n0_ref[h:h + 1, :]
        m0 = m0_all[:, h:h + 1]
        i_t = gates[:, h:h + 1]
        logf = _log_sigmoid(gates[:, NH_A + h:NH_A + h + 1])

        log_inter = m0 + logf
        m_t = jnp.maximum(log_inter, i_t)
        d = jnp.exp(i_t - m_t)
        w_inter = jnp.exp(log_inter - m_t)
        sw = jnp.sum(q * ks, axis=1, keepdims=True) * d
        v_col = jnp.sum(jnp.where(eye, v, 0.0), axis=1, keepdims=True)
        cq_col = jnp.sum(c0 * q, axis=1, keepdims=True)
        den = sw + w_inter * jnp.sum(n0 * q, axis=1, keepdims=True)
        num_col = sw * v_col + w_inter * cq_col
        h_col = num_col / jnp.maximum(jnp.abs(den), jnp.exp(-m_t))
        hh = jnp.sum(jnp.where(eye, h_col, 0.0), axis=0, keepdims=True)

        c_ref[h] = w_inter * c0 + (d * v_col) * ks
        n_ref[h:h + 1, :] = w_inter * n0 + d * ks
        m_ref[h:h + 1, :] = jnp.broadcast_to(m_t, (1, LANES))
        h_ref[:, cols] = _head_out(hh, g_ref[:, cols], o_ref[row, cols], z_ref[row, cols])


def _mlstm_sample(p, gates, bias_if, hnorm_g, cols, c0, n0, m0):
    db = p.shape[0]
    w_a = cols.w_a
    dh = w_a // NH_A

    def pspec(off):
        return pl.BlockSpec((db, w_a), lambda b, off=off: (0, off // w_a))

    return pl.pallas_call(
        _mlstm_sample_kernel,
        grid=(db,),
        in_specs=[pspec(cols.qa), pspec(cols.ka), pspec(cols.va), pspec(cols.oa), pspec(cols.za),
                  pl.BlockSpec((db, LANES), lambda b: (0, 0)),
                  pl.BlockSpec((1, LANES), lambda b: (0, 0)),
                  pl.BlockSpec((1, w_a), lambda b: (0, 0)),
                  pl.BlockSpec((None, NH_A, dh, dh), lambda b: (b, 0, 0, 0)),
                  pl.BlockSpec((None, NH_A, dh), lambda b: (b, 0, 0)),
                  pl.BlockSpec((db, NH_A), lambda b: (0, 0))],
        out_specs=[
            pl.BlockSpec((None, 1, w_a), lambda b: (b, 0, 0)),
            pl.BlockSpec((None, NH_A, dh, dh), lambda b: (b, 0, 0, 0)),
            pl.BlockSpec((None, NH_A, dh), lambda b: (b, 0, 0)),
            pl.BlockSpec((None, NH_A, LANES), lambda b: (b, 0, 0)),
        ],
        out_shape=[
            jax.ShapeDtypeStruct((db, 1, w_a), F32),
            jax.ShapeDtypeStruct((db, NH_A, dh, dh), F32),
            jax.ShapeDtypeStruct((db, NH_A, dh), F32),
            jax.ShapeDtypeStruct((db, NH_A, LANES), F32),
        ],
        compiler_params=_cparams("arbitrary"),
        name="mlstm_sample",
    )(p, p, p, p, p, gates, bias_if, hnorm_g.reshape(1, w_a), c0, n0, m0)


def _rms_rows(x, g):
    return x * lax.rsqrt(jnp.mean(x * x, axis=-1, keepdims=True) + RMS_EPS) * g


def _split_bf16(x):
    hi = x.astype(BF16)
    lo = (x - hi.astype(F32)).astype(BF16)
    return hi, lo


def _moba_prompt_kernel(q_ref, k_ref, v_ref, z_ref, qg_ref, kg_ref, kn_ref, vo_ref, hb_ref, km_scr):
    seq, dh = q_ref.shape
    nb = seq // BLOCK
    q = _rms_rows(q_ref[...], qg_ref[...])
    k = _rms_rows(k_ref[...], kg_ref[...])
    v = v_ref[...]
    kn_ref[...] = k
    vo_ref[...] = v
    kb = k.astype(BF16)
    vb = v.astype(BF16)
    qs = (q * (dh ** -0.5)).astype(BF16)

    km_scr[...] = jnp.zeros_like(km_scr)
    for n in range(nb - 1):
        km_scr[n:n + 1, :] = jnp.sum(k[n * BLOCK:(n + 1) * BLOCK], axis=0, keepdims=True) * (1.0 / BLOCK)
    km_hi, km_lo = _split_bf16(km_scr[...])

    t_idx = lax.broadcasted_iota(jnp.int32, (BLOCK, BLOCK), 0)
    s_idx = lax.broadcasted_iota(jnp.int32, (BLOCK, BLOCK), 1)
    causal = s_idx <= t_idx
    lane = lax.broadcasted_iota(jnp.int32, (BLOCK, LANES), 1)

    for j in range(nb):
        rows = slice(j * BLOCK, (j + 1) * BLOCK)
        qj = qs[rows]
        q_hi, q_lo = _split_bf16(q[rows])
        sc = _dot_nt(q_hi, km_hi) + (_dot_nt(q_hi, km_lo) + _dot_nt(q_lo, km_hi))
        rank = jnp.zeros((BLOCK, LANES), jnp.int32)
        for n2 in range(j):
            s2 = sc[:, n2:n2 + 1]
            beats = (s2 > sc) | ((s2 == sc) & (n2 < lane))
            rank = rank + beats.astype(jnp.int32)
        sel = (lane < j) & (rank < TOP_K)

        s_blocks = []
        for n in range(j + 1):
            s_n = _dot_nt(qj, kb[n * BLOCK:(n + 1) * BLOCK])
            mask = causal if n == j else sel[:, n:n + 1]
            s_blocks.append(jnp.where(mask, s_n, NEG_INF))
        mx = s_blocks[0].max(axis=1, keepdims=True)
        for s_n in s_blocks[1:]:
            mx = jnp.maximum(mx, s_n.max(axis=1, keepdims=True))
        acc = jnp.zeros((BLOCK, dh), F32)
        den = jnp.zeros((BLOCK, 1), F32)
        for n, s_n in enumerate(s_blocks):
            p_n = jnp.exp(s_n - mx)
            den = den + jnp.sum(p_n, axis=1, keepdims=True)
            acc = acc + _dot(p_n.astype(BF16), vb[n * BLOCK:(n + 1) * BLOCK])
        z = z_ref[rows, :]
        hb_ref[rows, :] = (acc / den) * (z * _sigmoid(z))


def _moba_prompt(p, q_norm_g, k_norm_g, cols, batch, seq):
    dh = cols.w_b // NH_B
    assert dh == LANES and seq % BLOCK == 0

    def pspec(off):
        return pl.BlockSpec((seq, dh), lambda b, h, off=off: (b, off // dh + h))

    ospec = pl.BlockSpec((seq, dh), lambda b, h: (b, h))
    oshape = jax.ShapeDtypeStruct((batch * seq, cols.w_b), F32)
    return pl.pallas_call(
        _moba_prompt_kernel,
        grid=(batch, NH_B),
        in_specs=[pspec(cols.qb), pspec(cols.kb), pspec(cols.vb), pspec(cols.zb),
                  pl.BlockSpec((1, dh), lambda b, h: (0, 0)),
                  pl.BlockSpec((1, dh), lambda b, h: (0, 0))],
        out_specs=[ospec, ospec, ospec],
        out_shape=[oshape, oshape, oshape],
        scratch_shapes=[pltpu.VMEM((LANES, dh), F32)],
        compiler_params=_cparams("parallel", "parallel"),
        name="moba_prompt",
    )(p, p, p, p, q_norm_g.reshape(1, dh), k_norm_g.reshape(1, dh))


_RING = 8


def _moba_select_kernel(pt_ref, q_ref, k_ref, qg_ref, kg_ref, cache_ref, qn_ref, kn_ref, top_ref,
                        ring, sems, pm_scr, *, n_pages):
    b = pl.program_id(0)
    nb_total = pl.num_programs(0) * n_pages
    ppb = BLOCK // PAGE_SIZE
    nblk = n_pages // ppb

    def page_copy(t, slot):
        return pltpu.make_async_copy(cache_ref.at[pt_ref[t]], ring.at[slot], sems.at[slot])

    @pl.when(b == 0)
    def _():
        for t in range(_RING - 1):
            page_copy(t, t).start()

    def body(pg, carry):
        t = b * n_pages + pg
        slot = lax.rem(t, _RING)
        nxt = t + (_RING - 1)

        @pl.when(nxt < nb_total)
        def _():
            page_copy(nxt, lax.rem(nxt, _RING)).start()

        page_copy(t, slot).wait()
        pm_scr[pg] = jnp.sum(ring[slot], axis=0)
        return carry

    lax.fori_loop(0, n_pages, body, 0)

    q = _rms_rows(q_ref[...], qg_ref[...])
    qn_ref[...] = q
    kn_ref[...] = _rms_rows(k_ref[...], kg_ref[...])

    pm = pm_scr[...].reshape(nblk, ppb, NH_B, pm_scr.shape[-1])
    kmean = jnp.sum(pm, axis=1) * (1.0 / BLOCK)
    sc = jnp.sum(kmean * q[None], axis=-1)
    blk = lax.broadcasted_iota(jnp.int32, sc.shape, 0)
    top_ref[...] = jnp.zeros_like(top_ref)
    for kk in range(TOP_K):
        mx = jnp.max(sc, axis=0, keepdims=True)
        idx = jnp.min(jnp.where(sc == mx, blk, nblk), axis=0, keepdims=True)
        top_ref[kk:kk + 1, 0:NH_B] = idx
        sc = jnp.where(blk == idx, NEG_INF, sc)


def _moba_select(pt_flat, q3, k3, q_norm_g, k_norm_g, cache_k, n_pages):
    db, nh, dh = q3.shape
    assert n_pages % (BLOCK // PAGE_SIZE) == 0 and n_pages // (BLOCK // PAGE_SIZE) >= TOP_K
    hspec = pl.BlockSpec((None, nh, dh), lambda b, pt: (b, 0, 0))
    gspec = pl.BlockSpec((1, dh), lambda b, pt: (0, 0))
    grid_spec = pltpu.PrefetchScalarGridSpec(
        num_scalar_prefetch=1,
        grid=(db,),
        in_specs=[hspec, hspec, gspec, gspec, pl.BlockSpec(memory_space=pl.ANY)],
        out_specs=[hspec, hspec, pl.BlockSpec((None, 8, LANES), lambda b, pt: (b, 0, 0))],
        scratch_shapes=[pltpu.VMEM((_RING, PAGE_SIZE, nh, dh), F32),
                        pltpu.SemaphoreType.DMA((_RING,)),
                        pltpu.VMEM((n_pages, nh, dh), F32)],
    )
    return pl.pallas_call(
        functools.partial(_moba_select_kernel, n_pages=n_pages),
        grid_spec=grid_spec,
        out_shape=[jax.ShapeDtypeStruct((db, nh, dh), F32), jax.ShapeDtypeStruct((db, nh, dh), F32),
                   jax.ShapeDtypeStruct((db, 8, LANES), jnp.int32)],
        compiler_params=_cparams("arbitrary"),
        name="moba_select",
    )(pt_flat, q3, k3, q_norm_g.reshape(1, dh), k_norm_g.reshape(1, dh), cache_k)


def _moba_attend_kernel(pt_ref, top_ref, q_ref, k_ref, v_ref, z_ref, ck_ref, cv_ref, hb_ref,
                        kbuf, vbuf, sem, *, n_pages):
    b = pl.program_id(0)
    ppb = BLOCK // PAGE_SIZE
    nsel = TOP_K * ppb
    dh = q_ref.shape[-1]

    def copies(h, kk, j):
        blk = top_ref[(b * TOP_K + kk) * NH_B + h]
        page = pt_ref[b * n_pages + blk * ppb + j]
        i = kk * ppb + j
        return (pltpu.make_async_copy(ck_ref.at[page, :, h, :], kbuf.at[h, i], sem.at[0]),
                pltpu.make_async_copy(cv_ref.at[page, :, h, :], vbuf.at[h, i], sem.at[1]))

    for h in range(NH_B):
        for kk in range(TOP_K):
            for j in range(ppb):
                ck, cv = copies(h, kk, j)
                ck.start()
                cv.start()
    for h in range(NH_B):
        for kk in range(TOP_K):
            for j in range(ppb):
                ck, cv = copies(h, kk, j)
                ck.wait()
                cv.wait()

    scale = dh ** -0.5
    for h in range(NH_B):
        q = q_ref[h:h + 1, :] * scale
        k_new = k_ref[h:h + 1, :]
        v_new = v_ref[h:h + 1, :]
        ks = kbuf[h].reshape(nsel * PAGE_SIZE, dh)
        vs = vbuf[h].reshape(nsel * PAGE_SIZE, dh)
        s_sel = jnp.sum(ks * q, axis=1, keepdims=True)
        s_loc = jnp.sum(k_new * q, axis=1, keepdims=True)
        mx = jnp.maximum(jnp.max(s_sel, axis=0, keepdims=True), s_loc)
        p_sel = jnp.exp(s_sel - mx)
        p_loc = jnp.exp(s_loc - mx)
        den = jnp.sum(p_sel, axis=0, keepdims=True) + p_loc
        out = (jnp.sum(p_sel * vs, axis=0, keepdims=True) + p_loc * v_new) / den
        z = z_ref[h:h + 1, :]
        hb_ref[h:h + 1, :] = out * (z * _sigmoid(z))


def _moba_attend(pt_flat, top_flat, qn, kn, v3, z3, cache_k, cache_v, n_pages):
    db, nh, dh = qn.shape
    nsel = TOP_K * (BLOCK // PAGE_SIZE)
    hspec = pl.BlockSpec((None, nh, dh), lambda b, pt, top: (b, 0, 0))
    anyspec = pl.BlockSpec(memory_space=pl.ANY)
    grid_spec = pltpu.PrefetchScalarGridSpec(
        num_scalar_prefetch=2,
        grid=(db,),
        in_specs=[hspec, hspec, hspec, hspec, anyspec, anyspec],
        out_specs=hspec,
        scratch_shapes=[pltpu.VMEM((nh, nsel, PAGE_SIZE, dh), F32),
                        pltpu.VMEM((nh, nsel, PAGE_SIZE, dh), F32),
                        pltpu.SemaphoreType.DMA((2,))],
    )
    return pl.pallas_call(
        functools.partial(_moba_attend_kernel, n_pages=n_pages),
        grid_spec=grid_spec,
        out_shape=jax.ShapeDtypeStruct((db, nh, dh), F32),
        compiler_params=_cparams("arbitrary"),
        name="moba_attend",
    )(pt_flat, top_flat, qn, kn, v3, z3, cache_k, cache_v)


def _out_proj_kernel(ha_ref, hb_ref, ga_ref, gb_ref, x_ref, wa_ref, wb_ref, wo_ref, y_ref):
    u = (_sigmoid(ga_ref[...]) * _dot(ha_ref[...].astype(BF16), wa_ref[...])
         + _sigmoid(gb_ref[...]) * _dot(hb_ref[...].astype(BF16), wb_ref[...]))
    y_ref[...] = x_ref[...] + _dot(u.astype(BF16), wo_ref[...])


def _out_proj(ha, hb, p, x2d, w_a, w_b, w_out, cols, tm):
    m, d = x2d.shape

    def wspec(shape):
        return pl.BlockSpec(shape, lambda i: (0, 0), pipeline_mode=pl.Buffered(1))

    return pl.pallas_call(
        _out_proj_kernel,
        grid=(m // tm,),
        in_specs=[
            pl.BlockSpec((tm, cols.w_a), lambda i: (i, 0)),
            pl.BlockSpec((tm, cols.w_b), lambda i: (i, 0)),
            pl.BlockSpec((tm, d), lambda i: (i, cols.ga // d)),
            pl.BlockSpec((tm, d), lambda i: (i, cols.gb // d)),
            pl.BlockSpec((tm, d), lambda i: (i, 0)),
            wspec((cols.w_a, d)), wspec((cols.w_b, d)), wspec((d, d)),
        ],
        out_specs=pl.BlockSpec((tm, d), lambda i: (i, 0)),
        out_shape=jax.ShapeDtypeStruct((m, d), F32),
        compiler_params=_cparams("parallel"),
        name="out_proj",
    )(ha, hb, p, p, x2d, w_a, w_b, w_out)


def _pick_tile(m, pref):
    t = min(m, pref)
    while m % t:
        t //= 2
    return t


def kernel(x_prompt, x_sample, cache_k, cache_v, state_c, state_n, state_m, page_table,
           norm_g, w_in, b_i, b_f, mlstm_norm_g, q_norm_g, k_norm_g, w_a, w_b, w_out):
    depth = w_in.shape[0]
    batch, seq, d = x_prompt.shape
    db, ds, _ = x_sample.shape
    n_pages = page_table.shape[1]
    assert depth == 1 and ds == 1, "single layer, one new token per running sequence"
    assert cache_k.shape[2] == PAGE_SIZE and cache_k.shape[3] == NH_B
    cols = _Cols(d)
    dh_b = cols.w_b // NH_B
    l = 0

    w_main, w_if = _prep_w_in(w_in[l], d)
    wa16, wb16, wo16 = w_a[l].astype(BF16), w_b[l].astype(BF16), w_out[l].astype(BF16)
    bias_if = jnp.pad(jnp.concatenate([b_i[l], b_f[l]]), (0, LANES - 2 * NH_A)).reshape(1, LANES)
    pt_flat = page_table.reshape(-1)

    xp = x_prompt.reshape(batch * seq, d)
    tn = _pick_tile(cols.n, 1024)
    p_p, if_p = _in_proj(xp, norm_g[l], w_main, w_if, _pick_tile(batch * seq, 1024), tn)
    ha_p, c_p, n_p, m_p = _mlstm_prompt(p_p, if_p, bias_if, mlstm_norm_g[l], cols, batch, seq)
    kn_p, v_p, hb_p = _moba_prompt(p_p, q_norm_g[l], k_norm_g[l], cols, batch, seq)
    y_p = _out_proj(ha_p, hb_p, p_p, xp, wa16, wb16, wo16, cols, _pick_tile(batch * seq, 256))

    xs = x_sample.reshape(db, d)
    p_s, if_s = _in_proj(xs, norm_g[l], w_main, w_if, db, tn)
    ha_s, c_s, n_s, m_s = _mlstm_sample(p_s, if_s, bias_if, mlstm_norm_g[l], cols,
                                        state_c[l], state_n[l], state_m[l])
    heads = lambda off: p_s[:, off:off + cols.w_b].reshape(db, NH_B, dh_b)
    qn_s, kn_s, top = _moba_select(pt_flat, heads(cols.qb), heads(cols.kb), q_norm_g[l], k_norm_g[l],
                                   cache_k[l], n_pages)
    top_flat = top[:, :TOP_K, :NH_B].reshape(-1)
    v_s = heads(cols.vb)
    hb_s = _moba_attend(pt_flat, top_flat, qn_s, kn_s, v_s, heads(cols.zb), cache_k[l], cache_v[l], n_pages)
    y_s = _out_proj(ha_s.reshape(db, cols.w_a), hb_s.reshape(db, cols.w_b), p_s, xs, wa16, wb16, wo16, cols, db)

    dh_a = cols.w_a // NH_A
    return (y_p.reshape(batch, seq, d), y_s.reshape(db, ds, d),
            kn_p.reshape(1, batch, seq, NH_B, dh_b), v_p.reshape(1, batch, seq, NH_B, dh_b),
            c_p.reshape(1, batch, NH_A, dh_a, dh_a), n_p.reshape(1, batch, NH_A, dh_a), m_p[:, :, 0, 0].reshape(1, batch, NH_A),
            kn_s.reshape(1, db, ds, NH_B, dh_b), v_s.reshape(1, db, ds, NH_B, dh_b),
            c_s.reshape(1, db, NH_A, dh_a, dh_a), n_s.reshape(1, db, NH_A, dh_a), m_s[:, :, 0].reshape(1, db, NH_A))
```
